```python
import jax, jax.numpy as jnp
from jax import lax
import numpy as np

D_MODEL = 1024
BATCH = 8
SEQ = 8192
DEPTH = 1

FFN_HIDDEN = 2816
H_A = 4
DK_A = 128
DV_A = 256
CONV_W = 4
CHUNK = 64
H_B = 8
DH_B = 128
SB_BLOCK = 128
PLE_DIM = 256
LN_EPS = 1e-5
DEEPNORM_ALPHA = (2 * DEPTH) ** 0.25
DEEPNORM_BETA = (8 * DEPTH) ** -0.25

A_QK_W = 2 * H_A * DK_A
A_V_W = H_A * DV_A
A_O_W = H_A * DV_A
B_QKV_W = 3 * H_B * DH_B
GATE_W = 2 * D_MODEL
SPLIT_IDX = list(np.cumsum([A_QK_W, A_V_W, A_O_W, H_A, H_A, B_QKV_W]))
N_IN = A_QK_W + A_V_W + A_O_W + 2 * H_A + B_QKV_W + GATE_W

kernel_name = "hybrid_mlstm_stickbreaking_macaron_deepnorm"


def layer_norm(x, g, b):
    xf = x.astype(jnp.float32)
    mu = jnp.mean(xf, axis=-1, keepdims=True)
    var = jnp.mean(jnp.square(xf - mu), axis=-1, keepdims=True)
    y = (xf - mu) * lax.rsqrt(var + LN_EPS) * g.astype(jnp.float32) + b.astype(jnp.float32)
    return y.astype(x.dtype)


def swiglu(x, wg, wu, wd):
    return (jax.nn.silu(x @ wg) * (x @ wu)) @ wd


def causal_conv(u, w, b):
    c = u.shape[-1]
    y = lax.conv_general_dilated(u, w[:, None, :], window_strides=(1,), padding=((CONV_W - 1, 0),),
                                 dimension_numbers=("NWC", "WIO", "NWC"), feature_group_count=c)
    return y + b


def head_norm(h, g):
    mu = jnp.mean(h, axis=-1, keepdims=True)
    var = jnp.mean(jnp.square(h - mu), axis=-1, keepdims=True)
    return (h - mu) * lax.rsqrt(var + LN_EPS) * g.astype(jnp.float32)


def mlstm_chunkwise(q, k, v, i_pre, log_f):
    bsz, s, h, dk = q.shape
    dv = v.shape[-1]
    nc = s // CHUNK
    f32 = jnp.float32
    qc = q.astype(f32).reshape(bsz, nc, CHUNK, h, dk).transpose(1, 0, 3, 2, 4)
    kc = k.astype(f32).reshape(bsz, nc, CHUNK, h, dk).transpose(1, 0, 3, 2, 4)
    vc = v.astype(f32).reshape(bsz, nc, CHUNK, h, dv).transpose(1, 0, 3, 2, 4)
    ic = i_pre.astype(f32).reshape(bsz, nc, CHUNK, h).transpose(1, 0, 3, 2)
    fc = log_f.astype(f32).reshape(bsz, nc, CHUNK, h).transpose(1, 0, 3, 2)
    tril = jnp.tril(jnp.ones((CHUNK, CHUNK), dtype=bool))

    def step(carry, inp):
        c_st, n_st, m_st = carry
        qb, kb, vb, ib, fb = inp
        bcum = jnp.cumsum(fb, axis=-1)
        dmat = jnp.where(tril, bcum[..., :, None] - bcum[..., None, :] + ib[..., None, :], -jnp.inf)
        inter = bcum + m_st[..., None]
        m_t = jnp.maximum(inter, jnp.max(dmat, axis=-1))
        w_inter = jnp.exp(inter - m_t)
        scores = jnp.einsum("bhtd,bhsd->bhts", qb, kb) * jnp.exp(dmat - m_t[..., None])
        num = w_inter[..., None] * jnp.einsum("bhtd,bhde->bhte", qb, c_st) + jnp.einsum("bhts,bhse->bhte", scores, vb)
        den = w_inter * jnp.einsum("bhtd,bhd->bht", qb, n_st) + jnp.sum(scores, axis=-1)
        h_out = num / jnp.maximum(jnp.abs(den), jnp.exp(-m_t))[..., None]
        b_last = bcum[..., -1]
        g = b_last[..., None] - bcum + ib
        m_new = jnp.maximum(b_last + m_st, jnp.max(g, axis=-1))
        a_prev = jnp.exp(b_last + m_st - m_new)
        w_in = jnp.exp(g - m_new[..., None])
        c_new = a_prev[..., None, None] * c_st + jnp.einsum("bhs,bhsd,bhse->bhde", w_in, kb, vb)
        n_new = a_prev[..., None] * n_st + jnp.einsum("bhs,bhsd->bhd", w_in, kb)
        return (c_new, n_new, m_new), h_out

    init = (jnp.zeros((bsz, h, dk, dv), f32), jnp.zeros((bsz, h, dk), f32), jnp.zeros((bsz, h), f32))
    _, hs = lax.scan(step, init, (qc, kc, vc, ic, fc))
    return hs.transpose(1, 0, 3, 2, 4).reshape(bsz, s, h, dv)


def stick_breaking_attention(q, k, v):
    bsz, s, h, dh = q.shape
    nq = s // SB_BLOCK
    scale = DH_B ** -0.5
    kt = k.transpose(0, 2, 1, 3)
    vt = v.transpose(0, 2, 1, 3)
    qblocks = q.transpose(0, 2, 1, 3).reshape(bsz, h, nq, SB_BLOCK, dh).transpose(2, 0, 1, 3, 4)
    s_pos = jnp.arange(s)

    def block(args):
        qb, bi = args
        z = jnp.einsum("bhtd,bhsd->bhts", qb, kt).astype(jnp.float32) * scale
        t_pos = bi * SB_BLOCK + jnp.arange(SB_BLOCK)
        mask = s_pos[None, :] < t_pos[:, None]
        log_keep = jnp.where(mask, -jax.nn.softplus(z), 0.0)
        later = lax.cumsum(log_keep, axis=3, reverse=True) - log_keep
        weights = jnp.where(mask, jnp.exp(jax.nn.log_sigmoid(z) + later), 0.0)
        return jnp.einsum("bhts,bhsd->bhtd", weights.astype(vt.dtype), vt)

    out = lax.map(block, (qblocks, jnp.arange(nq)))
    return out.transpose(1, 0, 3, 2, 4).reshape(bsz, s, h * dh)


def token_mix(x, w_in, b_in, b_forget, conv_w, conv_b, mh_norm_g, w_branch_a, w_branch_b, w_mix_out):
    bsz, s, _ = x.shape
    proj = x @ w_in + b_in
    qk_a, v_a, o_a, i_a, f_a, qkv_b, gates = jnp.split(proj, SPLIT_IDX, axis=-1)
    qk_a = jax.nn.silu(causal_conv(qk_a, conv_w, conv_b))
    q_a = qk_a[..., : A_QK_W // 2].reshape(bsz, s, H_A, DK_A)
    k_a = qk_a[..., A_QK_W // 2:].reshape(bsz, s, H_A, DK_A) * (DK_A ** -0.5)
    v_a = v_a.reshape(bsz, s, H_A, DV_A)
    log_f = jax.nn.log_sigmoid((f_a + b_forget).astype(jnp.float32))
    h_a = mlstm_chunkwise(q_a, k_a, v_a, i_a, log_f)
    y_a = (head_norm(h_a, mh_norm_g).reshape(bsz, s, A_V_W) * jax.nn.sigmoid(o_a.astype(jnp.float32))).astype(x.dtype)
    q_b, k_b, v_b = jnp.split(qkv_b, 3, axis=-1)
    y_b = stick_breaking_attention(q_b.reshape(bsz, s, H_B, DH_B), k_b.reshape(bsz, s, H_B, DH_B),
                                   v_b.reshape(bsz, s, H_B, DH_B)).astype(x.dtype)
    g_a, g_b = jnp.split(gates, 2, axis=-1)
    merged = jax.nn.sigmoid(g_a) * (y_a @ w_branch_a) + jax.nn.sigmoid(g_b) * (y_b @ w_branch_b)
    return merged @ w_mix_out


def setup_inputs(seed: int = 0) -> dict:
    key = jax.random.key(seed)
    ks = jax.random.split(key, 20)
    f32 = jnp.float32

    def dense(k, shape, fan_in, scale=1.0):
        return jax.random.normal(k, shape, f32) * (scale * fan_in ** -0.5)

    x = jax.random.normal(ks[0], (BATCH, SEQ, D_MODEL), f32)
    p = jax.random.normal(ks[1], (DEPTH, BATCH, SEQ, PLE_DIM), f32)
    ln_g = 1.0 + 0.02 * jax.random.normal(ks[2], (DEPTH, 4, D_MODEL), f32)
    ln_b = 0.02 * jax.random.normal(ks[3], (DEPTH, 4, D_MODEL), f32)
    ffn_wg = dense(ks[4], (DEPTH, 2, D_MODEL, FFN_HIDDEN), D_MODEL)
    ffn_wu = dense(ks[5], (DEPTH, 2, D_MODEL, FFN_HIDDEN), D_MODEL)
    ffn_wd = dense(ks[6], (DEPTH, 2, FFN_HIDDEN, D_MODEL), FFN_HIDDEN, DEEPNORM_BETA)
    w_in = dense(ks[7], (DEPTH, D_MODEL, N_IN), D_MODEL)
    b_in = 0.01 * jax.random.normal(ks[8], (DEPTH, N_IN), f32)
    b_forget = jnp.linspace(3.0, 6.0, H_A, dtype=f32)[None, :] + 0.1 * jax.random.normal(ks[9], (DEPTH, H_A), f32)
    conv_w = dense(ks[10], (DEPTH, CONV_W, A_QK_W), CONV_W)
    conv_b = 0.01 * jax.random.normal(ks[11], (DEPTH, A_QK_W), f32)
    mh_norm_g = 1.0 + 0.02 * jax.random.normal(ks[12], (DEPTH, H_A, DV_A), f32)
    w_branch_a = dense(ks[13], (DEPTH, A_V_W, D_MODEL), A_V_W)
    w_branch_b = dense(ks[14], (DEPTH, H_B * DH_B, D_MODEL), H_B * DH_B)
    w_mix_out = dense(ks[15], (DEPTH, D_MODEL, D_MODEL), D_MODEL, DEEPNORM_BETA)
    w_ple = dense(ks[16], (DEPTH, PLE_DIM, D_MODEL), PLE_DIM, DEEPNORM_BETA)
    w_ple_gate = dense(ks[17], (DEPTH, D_MODEL, D_MODEL), D_MODEL)
    b_ple_gate = 0.01 * jax.random.normal(ks[18], (DEPTH, D_MODEL), f32)
    return {"x": x, "p": p, "ln_g": ln_g, "ln_b": ln_b, "ffn_wg": ffn_wg, "ffn_wu": ffn_wu,
            "ffn_wd": ffn_wd, "w_in": w_in, "b_in": b_in, "b_forget": b_forget, "conv_w": conv_w,
            "conv_b": conv_b, "mh_norm_g": mh_norm_g, "w_branch_a": w_branch_a,
            "w_branch_b": w_branch_b, "w_mix_out": w_mix_out, "w_ple": w_ple,
            "w_ple_gate": w_ple_gate, "b_ple_gate": b_ple_gate}


def reference(x, p, ln_g, ln_b, ffn_wg, ffn_wu, ffn_wd, w_in, b_in, b_forget, conv_w, conv_b,
              mh_norm_g, w_branch_a, w_branch_b, w_mix_out, w_ple, w_ple_gate, b_ple_gate):
    for i in range(DEPTH):
        x = layer_norm(DEEPNORM_ALPHA * x + 0.5 * swiglu(x, ffn_wg[i, 0], ffn_wu[i, 0], ffn_wd[i, 0]),
                       ln_g[i, 0], ln_b[i, 0])
        mix = token_mix(x, w_in[i], b_in[i], b_forget[i], conv_w[i], conv_b[i], mh_norm_g[i],
                        w_branch_a[i], w_branch_b[i], w_mix_out[i])
        x = layer_norm(DEEPNORM_ALPHA * x + mix, ln_g[i, 1], ln_b[i, 1])
        x = layer_norm(DEEPNORM_ALPHA * x + 0.5 * swiglu(x, ffn_wg[i, 1], ffn_wu[i, 1], ffn_wd[i, 1]),
                       ln_g[i, 2], ln_b[i, 2])
        gate = jax.nn.sigmoid(x @ w_ple_gate[i] + b_ple_gate[i])
        x = layer_norm(DEEPNORM_ALPHA * x + gate * (p[i] @ w_ple[i]), ln_g[i, 3], ln_b[i, 3])
    return x
```

```python
import functools

import jax
import jax.numpy as jnp
from jax import lax
from jax.experimental import pallas as pl
from jax.experimental.pallas import tpu as pltpu

F32 = jnp.float32
BF16 = jnp.bfloat16

DEPTH = 1
H_A, DK_A, DV_A = 4, 128, 256
CONV_W = 4
H_B, DH_B = 8, 128
LN_EPS = 1e-5
ALPHA = (2 * DEPTH) ** 0.25

V7X_VMEM_BYTES = 64 * 1024 * 1024
V7X_LANES = 128
V7X_SUBLANES = 8
VMEM_LIMIT = V7X_VMEM_BYTES - 8 * 1024 * 1024

FFN_TM = 1024
FFN_CHUNK = 256
PROJ_TM = 512
MERGE_TM = 512
MLSTM_L = 128
MLSTM_TS = 256
ATT_TQ = 512
ATT_TK = 256


def _dot(a, b):
    return jnp.dot(a, b, preferred_element_type=F32)


def _dot_nt(a, b):
    return lax.dot_general(a, b, (((1,), (1,)), ((), ())), preferred_element_type=F32)


def _dot_tn(a, b):
    return lax.dot_general(a, b, (((0,), (0,)), ((), ())), preferred_element_type=F32)


def _layer_norm(r, g, b):
    mu = jnp.mean(r, axis=-1, keepdims=True)
    c = r - mu
    var = jnp.mean(c * c, axis=-1, keepdims=True)
    return c * lax.rsqrt(var + LN_EPS) * g + b


def _log_sigmoid(x):
    return jnp.minimum(x, 0.0) - jnp.log1p(jnp.exp(-jnp.abs(x)))


def _split3(x):
    hi = x.astype(BF16)
    r1 = x - hi.astype(F32)
    mid = r1.astype(BF16)
    lo = (r1 - mid.astype(F32)).astype(BF16)
    return hi, mid, lo


def _resident(shape):
    nd = len(shape)
    return pl.BlockSpec(shape, lambda *_: (0,) * nd, pipeline_mode=pl.Buffered(1))


def _ffn_ln_kernel(*refs, with_ple):
    if with_ple:
        (x_ref, wg_ref, wu_ref, wd_ref, g_ref, b_ref,
         p_ref, wp_ref, wpg_ref, bpg_ref, g2_ref, b2_ref, o_ref) = refs
    else:
        x_ref, wg_ref, wu_ref, wd_ref, g_ref, b_ref, o_ref = refs
    x = x_ref[...]
    xb = x.astype(BF16)
    hidden = wg_ref.shape[1]
    acc = None
    for c0 in range(0, hidden, FFN_CHUNK):
        gate = _dot(xb, wg_ref[:, c0:c0 + FFN_CHUNK])
        up = _dot(xb, wu_ref[:, c0:c0 + FFN_CHUNK])
        h = (gate * jax.nn.sigmoid(gate) * up).astype(BF16)
        y = _dot(h, wd_ref[c0:c0 + FFN_CHUNK, :])
        acc = y if acc is None else acc + y
    x1 = _layer_norm(ALPHA * x + 0.5 * acc, g_ref[...], b_ref[...])
    if with_ple:
        x1b = x1.astype(BF16)
        gate = jax.nn.sigmoid(_dot(x1b, wpg_ref[...]) + bpg_ref[...])
        emb = _dot(p_ref[...].astype(BF16), wp_ref[...])
        x1 = _layer_norm(ALPHA * x1 + gate * emb, g2_ref[...], b2_ref[...])
    o_ref[...] = x1


def _ffn_ln(x, wg, wu, wd, g, b, ple=None):
    t, d = x.shape
    hidden = wg.shape[1]
    assert t % FFN_TM == 0 and hidden % FFN_CHUNK == 0
    row = lambda w: pl.BlockSpec((FFN_TM, w), lambda i: (i, 0))
    in_specs = [row(d), _resident((d, hidden)), _resident((d, hidden)), _resident((hidden, d)),
                _resident((1, d)), _resident((1, d))]
    args = [x, wg, wu, wd, g, b]
    if ple is not None:
        p, wp, wpg, bpg, g2, b2 = ple
        pd = p.shape[1]
        in_specs += [row(pd), _resident((pd, d)), _resident((d, d)), _resident((1, d)),
                     _resident((1, d)), _resident((1, d))]
        args += [p, wp, wpg, bpg, g2, b2]
    return pl.pallas_call(
        functools.partial(_ffn_ln_kernel, with_ple=ple is not None),
        grid=(t // FFN_TM,),
        in_specs=in_specs,
        out_specs=row(d),
        out_shape=jax.ShapeDtypeStruct((t, d), F32),
        compiler_params=pltpu.CompilerParams(
            dimension_semantics=("arbitrary",), vmem_limit_bytes=VMEM_LIMIT),
        name="ffn_ln_ple" if ple is not None else "ffn_ln",
    )(*args)


_PROJ_GROUPS = ("qk_a", "v_a", "o_a", "q_b", "k_b", "v_b", "g_a", "g_b")
_PROJ_DTYPES = {"qk_a": F32, "v_a": BF16, "o_a": F32, "q_b": BF16, "k_b": BF16, "v_b": BF16,
                "g_a": F32, "g_b": F32}


def _in_proj_kernel(x_ref, w_ref, b_ref, wif_ref, bif_ref, wift_ref, bift_ref, *out_refs):
    xb = x_ref[...].astype(BF16)
    d = x_ref.shape[1]
    for gi, o_ref in enumerate(out_refs[:len(_PROJ_GROUPS)]):
        c0 = gi * d
        y = _dot(xb, w_ref[:, c0:c0 + d]) + b_ref[:, c0:c0 + d]
        o_ref[...] = y.astype(o_ref.dtype)
    ifc_ref, ifr_ref = out_refs[len(_PROJ_GROUPS):]
    ifc_ref[...] = _dot(xb, wif_ref[...]) + bif_ref[...]
    ifr_ref[...] = _dot_nt(wift_ref[...], xb) + bift_ref[...]


def _in_proj(x, w_main, b_main, w_if, b_if, w_ift, b_ift):
    t, d = x.shape
    assert t % PROJ_TM == 0
    n_main = w_main.shape[1]
    row = lambda w: pl.BlockSpec((PROJ_TM, w), lambda i: (i, 0))
    out_shape = [jax.ShapeDtypeStruct((t, d), _PROJ_DTYPES[k]) for k in _PROJ_GROUPS]
    out_specs = [row(d) for _ in _PROJ_GROUPS]
    out_shape += [jax.ShapeDtypeStruct((t, V7X_LANES), F32),
                  jax.ShapeDtypeStruct((V7X_SUBLANES, t), F32)]
    out_specs += [row(V7X_LANES), pl.BlockSpec((V7X_SUBLANES, PROJ_TM), lambda i: (0, i))]
    return pl.pallas_call(
        _in_proj_kernel,
        grid=(t // PROJ_TM,),
        in_specs=[row(d), _resident((d, n_main)), _resident((1, n_main)),
                  _resident((d, V7X_LANES)), _resident((1, V7X_LANES)),
                  _resident((V7X_SUBLANES, d)), _resident((V7X_SUBLANES, 1))],
        out_specs=out_specs,
        out_shape=out_shape,
        compiler_params=pltpu.CompilerParams(
            dimension_semantics=("arbitrary",), vmem_limit_bytes=VMEM_LIMIT),
        name="in_proj",
    )(x, w_main, b_main, w_if, b_if, w_ift, b_ift)


def _mlstm_kernel(qk_ref, v_ref, o_ref, ifc_ref, ifr_ref, cw_ref, cb_ref, bfc_ref, bfr_ref,
                  hg_ref, y_ref, ext_ref, q_s, k_s, c_s, n_s, m_s):
    ts = qk_ref.shape[1]
    lc = MLSTM_L
    halo = V7X_SUBLANES

    @pl.when(pl.program_id(1) == 0)
    def _():
        ext_ref[0:halo, :] = jnp.zeros((halo, ext_ref.shape[1]), F32)
        c_s[...] = jnp.zeros(c_s.shape, F32)
        n_s[...] = jnp.zeros(n_s.shape, F32)
        m_s[...] = jnp.zeros(m_s.shape, F32)

    ext_ref[halo:halo + ts, :] = qk_ref[0]
    conv = cb_ref[...]
    for j in range(CONV_W):
        off = halo - (CONV_W - 1) + j
        conv = conv + cw_ref[j:j + 1, :] * ext_ref[off:off + ts, :]
    ext_ref[0:halo, :] = ext_ref[ts:ts + halo, :]
    qk = conv * jax.nn.sigmoid(conv)
    half = qk.shape[1] // 2
    q_s[...] = qk[:, :half]
    k_s[...] = qk[:, half:] * (DK_A ** -0.5)

    ifc = ifc_ref[...]
    ifr = ifr_ref[...]
    logf_c = _log_sigmoid(ifc + bfc_ref[...])
    logf_r = _log_sigmoid(ifr + bfr_ref[...])

    t_ids = lax.broadcasted_iota(jnp.int32, (lc, lc), 0)
    s_ids = lax.broadcasted_iota(jnp.int32, (lc, lc), 1)
    causal = s_ids <= t_ids
    lower = causal.astype(BF16)
    upper = (t_ids <= s_ids).astype(BF16)

    for c in range(ts // lc):
        r0 = c * lc
        hi, mid, lo = _split3(logf_c[r0:r0 + lc, :])
        bcum_c = _dot(lower, hi) + _dot(lower, mid) + _dot(lower, lo)
        hi, mid, lo = _split3(logf_r[:, r0:r0 + lc])
        bcum_r = _dot(hi, upper) + _dot(mid, upper) + _dot(lo, upper)
        for h in range(H_A):
            q = q_s[r0:r0 + lc, h * DK_A:(h + 1) * DK_A]
            k = k_s[r0:r0 + lc, h * DK_A:(h + 1) * DK_A]
            v = v_ref[0, r0:r0 + lc, h * DV_A:(h + 1) * DV_A]
            qb = q.astype(BF16)
            fh = H_A + h
            bc_col = bcum_c[:, fh:fh + 1]
            bc_row = bcum_r[fh:fh + 1, :]
            i_col = ifc[r0:r0 + lc, h:h + 1]
            i_row = ifr[h:h + 1, r0:r0 + lc]
            c_st = c_s[h]
            n_st = n_s[h]
            m_st = m_s[h, 0:1, 0:1]

            dmat = jnp.where(causal, bc_col - bc_row + i_row, -jnp.inf)
            inter = bc_col + m_st
            m_t = jnp.maximum(inter, jnp.max(dmat, axis=1, keepdims=True))
            w_inter = jnp.exp(inter - m_t)
            scores = _dot_nt(qb, k.astype(BF16)) * jnp.exp(dmat - m_t)
            num = w_inter * _dot(qb, c_st.astype(BF16)) + _dot(scores.astype(BF16), v)
            den = (w_inter * jnp.sum(q * n_st, axis=1, keepdims=True)
                   + jnp.sum(scores, axis=1, keepdims=True))
            h_out = num * (1.0 / jnp.maximum(jnp.abs(den), jnp.exp(-m_t)))

            mu = jnp.mean(h_out, axis=1, keepdims=True)
            cen = h_out - mu
            var = jnp.mean(cen * cen, axis=1, keepdims=True)
            hn = cen * lax.rsqrt(var + LN_EPS) * hg_ref[h:h + 1, :]
            gate = jax.nn.sigmoid(o_ref[0, r0:r0 + lc, h * DV_A:(h + 1) * DV_A])
            y_ref[0, r0:r0 + lc, h * DV_A:(h + 1) * DV_A] = (hn * gate).astype(y_ref.dtype)

            b_last = bc_col[lc - 1:lc, :]
            g_row = b_last - bc_row + i_row
            g_col = b_last - bc_col + i_col
            m_new = jnp.maximum(b_last + m_st, jnp.max(g_row, axis=1, keepdims=True))
            a_prev = jnp.exp(b_last + m_st - m_new)
            kw = k * jnp.exp(g_col - m_new)
            c_s[h] = a_prev * c_st + _dot_tn(kw.astype(BF16), v)
            n_s[h] = a_prev * n_st + jnp.sum(kw, axis=0, keepdims=True)
            m_s[h] = jnp.broadcast_to(m_new, m_s.shape[1:])


def _mlstm(qk_pre, v_a, o_a, ifc, ifr, conv_w, conv_b, bf_col, bf_row, hn_g):
    b, s, wqk = qk_pre.shape
    wv = v_a.shape[2]
    ts = MLSTM_TS
    assert s % ts == 0 and ts % MLSTM_L == 0
    nblk = s // ts
    blk = lambda w: pl.BlockSpec((1, ts, w), lambda bi, si: (bi, si, 0))
    const = lambda shape: pl.BlockSpec(shape, lambda bi, si: (0,) * len(shape))
    return pl.pallas_call(
        _mlstm_kernel,
        grid=(b, nblk),
        in_specs=[blk(wqk), blk(wv), blk(wv),
                  pl.BlockSpec((ts, V7X_LANES), lambda bi, si: (bi * nblk + si, 0)),
                  pl.BlockSpec((V7X_SUBLANES, ts), lambda bi, si: (0, bi * nblk + si)),
                  const(conv_w.shape), const(conv_b.shape), const(bf_col.shape),
                  const(bf_row.shape), const(hn_g.shape)],
        out_specs=blk(wv),
        out_shape=jax.ShapeDtypeStruct((b, s, wv), BF16),
        scratch_shapes=[
            pltpu.VMEM((ts + 2 * V7X_SUBLANES, wqk), F32),
            pltpu.VMEM((ts, wqk // 2), F32),
            pltpu.VMEM((ts, wqk // 2), F32),
            pltpu.VMEM((H_A, DK_A, DV_A), F32),
            pltpu.VMEM((H_A, 1, DK_A), F32),
            pltpu.VMEM((H_A, V7X_SUBLANES, V7X_LANES), F32),
        ],
        compiler_params=pltpu.CompilerParams(
            dimension_semantics=("arbitrary", "arbitrary"), vmem_limit_bytes=VMEM_LIMIT),
        name="mlstm",
    )(qk_pre, v_a, o_a, ifc, ifr, conv_w, conv_b, bf_col, bf_row, hn_g)


def _sb_attn_kernel(q_ref, k_ref, v_ref, u2_ref, y_ref, out_s, acc_s):
    tq, tk = ATT_TQ, ATT_TK
    qi = pl.program_id(2)
    scale = DH_B ** -0.5
    q = q_ref[0]
    out_s[...] = jnp.zeros(out_s.shape, F32)
    acc_s[...] = jnp.zeros(acc_s.shape, F32)
    kpb = tq // tk

    def tile(kj, masked):
        k0 = pl.multiple_of(kj * tk, tk)
        kb = k_ref[0, pl.ds(k0, tk), :]
        vb = v_ref[0, pl.ds(k0, tk), :]
        z = _dot_nt(q, kb) * scale
        sp = jnp.maximum(z, 0.0) + jnp.log1p(jnp.exp(-jnp.abs(z)))
        lk = -sp
        if masked:
            t_pos = qi * tq + lax.broadcasted_iota(jnp.int32, (tq, tk), 0)
            s_pos = k0 + lax.broadcasted_iota(jnp.int32, (tq, tk), 1)
            mask = s_pos < t_pos
            lk = jnp.where(mask, lk, 0.0)
        hi = lk.astype(BF16)
        lo = (lk - hi.astype(F32)).astype(BF16)
        later = _dot(jnp.concatenate([hi, lo], axis=1), u2_ref[...]) + acc_s[...]
        w = jnp.exp((z - sp) + later)
        if masked:
            w = jnp.where(mask, w, 0.0)
        out_s[...] += _dot(w.astype(BF16), vb)
        acc_s[...] += jnp.sum(lk, axis=1, keepdims=True)

    for d in range(kpb):
        tile(qi * kpb + (kpb - 1 - d), True)

    def body(j, carry):
        tile(qi * kpb - 1 - j, False)
        return carry

    lax.fori_loop(0, qi * kpb, body, 0)
    y_ref[0] = out_s[...].astype(y_ref.dtype)


def _sb_attn(q, k, v, u2):
    b, s, w = q.shape
    nh = w // DH_B
    assert s % ATT_TQ == 0 and ATT_TQ % ATT_TK == 0
    return pl.pallas_call(
        _sb_attn_kernel,
        grid=(b, nh, s // ATT_TQ),
        in_specs=[pl.BlockSpec((1, ATT_TQ, DH_B), lambda bi, hi, qi: (bi, qi, hi)),
                  pl.BlockSpec((1, s, DH_B), lambda bi, hi, qi: (bi, 0, hi)),
                  pl.BlockSpec((1, s, DH_B), lambda bi, hi, qi: (bi, 0, hi)),
                  pl.BlockSpec(u2.shape, lambda bi, hi, qi: (0, 0))],
        out_specs=pl.BlockSpec((1, ATT_TQ, DH_B), lambda bi, hi, qi: (bi, qi, hi)),
        out_shape=jax.ShapeDtypeStruct((b, s, w), BF16),
        scratch_shapes=[pltpu.VMEM((ATT_TQ, DH_B), F32), pltpu.VMEM((ATT_TQ, 1), F32)],
        compiler_params=pltpu.CompilerParams(
            dimension_semantics=("arbitrary", "arbitrary", "arbitrary"),
            vmem_limit_bytes=VMEM_LIMIT),
        name="sb_attn",
    )(q, k, v, u2)


def _merge_ln_kernel(x_ref, ya_ref, yb_ref, ga_ref, gb_ref, wa_ref, wb_ref, wo_ref, g_ref, b_ref,
                     o_ref):
    merged = (jax.nn.sigmoid(ga_ref[...]) * _dot(ya_ref[...], wa_ref[...])
              + jax.nn.sigmoid(gb_ref[...]) * _dot(yb_ref[...], wb_ref[...]))
    mix = _dot(merged.astype(BF16), wo_ref[...])
    o_ref[...] = _layer_norm(ALPHA * x_ref[...] + mix, g_ref[...], b_ref[...])


def _merge_ln(x, ya, yb, ga, gb, wa, wb, wo, g, b):
    t, d = x.shape
    assert t % MERGE_TM == 0
    row = pl.BlockSpec((MERGE_TM, d), lambda i: (i, 0))
    return pl.pallas_call(
        _merge_ln_kernel,
        grid=(t // MERGE_TM,),
        in_specs=[row] * 5 + [_resident((d, d))] * 3 + [_resident((1, d))] * 2,
        out_specs=row,
        out_shape=jax.ShapeDtypeStruct((t, d), F32),
        compiler_params=pltpu.CompilerParams(
            dimension_semantics=("arbitrary",), vmem_limit_bytes=VMEM_LIMIT),
        name="merge_ln",
    )(x, ya, yb, ga, gb, wa, wb, wo, g, b)


def _pad_cols(a, width):
    return jnp.pad(a, ((0, 0), (0, width - a.shape[1])))


def kernel(x, p, ln_g, ln_b, ffn_wg, ffn_wu, ffn_wd, w_in, b_in, b_forget, conv_w, conv_b, mh_norm_g, w_branch_a, w_branch_b, w_mix_out, w_ple, w_ple_gate, b_ple_gate):
    bsz, s, d = x.shape
    t = bsz * s
    a_qk, a_v = 2 * H_A * DK_A, H_A * DV_A
    n_if = 2 * H_A
    if0 = a_qk + 2 * a_v
    xt = x.reshape(t, d)

    tri = (jnp.arange(ATT_TK)[:, None] > jnp.arange(ATT_TK)[None, :]).astype(BF16)
    u2 = jnp.concatenate([tri, tri], axis=0)

    for i in range(DEPTH):
        vec = lambda a: a.reshape(1, -1)
        wbf = lambda a: a.astype(BF16)
        xt = _ffn_ln(xt, wbf(ffn_wg[i, 0]), wbf(ffn_wu[i, 0]), wbf(ffn_wd[i, 0]),
                     vec(ln_g[i, 0]), vec(ln_b[i, 0]))

        w_main = wbf(jnp.concatenate([w_in[i][:, :if0], w_in[i][:, if0 + n_if:]], axis=1))
        b_main = vec(jnp.concatenate([b_in[i][:if0], b_in[i][if0 + n_if:]]))
        w_if = w_in[i][:, if0:if0 + n_if]
        b_if = b_in[i][if0:if0 + n_if]
        proj = _in_proj(xt, w_main, b_main,
                        wbf(_pad_cols(w_if, V7X_LANES)), _pad_cols(vec(b_if), V7X_LANES),
                        wbf(w_if.T), b_if.reshape(n_if, 1))
        qk_a, v_a, o_a, q_b, k_b, v_b, g_a, g_b, ifc, ifr = proj

        zeros_h = jnp.zeros((H_A,), F32)
        bf_all = jnp.concatenate([zeros_h, b_forget[i]])
        seq = lambda a: a.reshape(bsz, s, a.shape[1])
        y_a = _mlstm(seq(qk_a), seq(v_a), seq(o_a), ifc, ifr, conv_w[i], vec(conv_b[i]),
                     _pad_cols(vec(bf_all), V7X_LANES), bf_all.reshape(n_if, 1), mh_norm_g[i])
        y_b = _sb_attn(seq(q_b), seq(k_b), seq(v_b), u2)

        xt = _merge_ln(xt, y_a.reshape(t, d), y_b.reshape(t, d), g_a, g_b,
                       wbf(w_branch_a[i]), wbf(w_branch_b[i]), wbf(w_mix_out[i]),
                       vec(ln_g[i, 1]), vec(ln_b[i, 1]))

        xt = _ffn_ln(xt, wbf(ffn_wg[i, 1]), wbf(ffn_wu[i, 1]), wbf(ffn_wd[i, 1]),
                     vec(ln_g[i, 2]), vec(ln_b[i, 2]),
                     ple=(p[i].reshape(t, -1), wbf(w_ple[i]), wbf(w_ple_gate[i]),
                          vec(b_ple_gate[i]), vec(ln_g[i, 3]), vec(ln_b[i, 3])))
    return xt.reshape(bsz, s, d)
```

```python
import functools

import jax
import jax.numpy as jnp
from jax import lax
from jax.experimental import pallas as pl
from jax.experimental.pallas import tpu as pltpu

F32 = jnp.float32
BF16 = jnp.bfloat16

DEPTH = 1
H_A, DK_A, DV_A = 4, 128, 256
CONV_W = 4
H_B, DH_B = 8, 128
LN_EPS = 1e-5
ALPHA = (2 * DEPTH) ** 0.25
LOG2_E = 1.4426950408889634
MASKED_LOG2 = -1e30

V7X_VMEM_BYTES = 64 * 1024 * 1024
V7X_LANES = 128
V7X_SUBLANES = 8
VMEM_LIMIT = V7X_VMEM_BYTES - 8 * 1024 * 1024

FFN_TM = 1024
FFN_CHUNK = 256
PROJ_TM = 512
MERGE_TM = 512
MLSTM_L = 128
MLSTM_TS = 256
ATT_TQ = 512
ATT_TK = 256


def _dot(a, b):
    return jnp.dot(a, b, preferred_element_type=F32)


def _dot_nt(a, b):
    return lax.dot_general(a, b, (((1,), (1,)), ((), ())), preferred_element_type=F32)


def _dot_tn(a, b):
    return lax.dot_general(a, b, (((0,), (0,)), ((), ())), preferred_element_type=F32)


def _layer_norm(r, g, b):
    mu = jnp.mean(r, axis=-1, keepdims=True)
    c = r - mu
    var = jnp.mean(c * c, axis=-1, keepdims=True)
    return c * lax.rsqrt(var + LN_EPS) * g + b


def _log_sigmoid(x):
    return jnp.minimum(x, 0.0) - jnp.log1p(jnp.exp(-jnp.abs(x)))


def _split3(x):
    hi = x.astype(BF16)
    r1 = x - hi.astype(F32)
    mid = r1.astype(BF16)
    lo = (r1 - mid.astype(F32)).astype(BF16)
    return hi, mid, lo


def _resident(shape):
    nd = len(shape)
    return pl.BlockSpec(shape, lambda *_: (0,) * nd, pipeline_mode=pl.Buffered(1))


def _ffn_ln_kernel(*refs, with_ple):
    if with_ple:
        (x_ref, wg_ref, wu_ref, wd_ref, g_ref, b_ref,
         p_ref, wp_ref, wpg_ref, bpg_ref, g2_ref, b2_ref, o_ref) = refs
    else:
        x_ref, wg_ref, wu_ref, wd_ref, g_ref, b_ref, o_ref = refs
    x = x_ref[...]
    xb = x.astype(BF16)
    hidden = wg_ref.shape[1]
    acc = None
    for c0 in range(0, hidden, FFN_CHUNK):
        gate = _dot(xb, wg_ref[:, c0:c0 + FFN_CHUNK])
        up = _dot(xb, wu_ref[:, c0:c0 + FFN_CHUNK])
        h = (gate * jax.nn.sigmoid(gate) * up).astype(BF16)
        y = _dot(h, wd_ref[c0:c0 + FFN_CHUNK, :])
        acc = y if acc is None else acc + y
    x1 = _layer_norm(ALPHA * x + 0.5 * acc, g_ref[...], b_ref[...])
    if with_ple:
        x1b = x1.astype(BF16)
        gate = jax.nn.sigmoid(_dot(x1b, wpg_ref[...]) + bpg_ref[...])
        emb = _dot(p_ref[...].astype(BF16), wp_ref[...])
        x1 = _layer_norm(ALPHA * x1 + gate * emb, g2_ref[...], b2_ref[...])
    o_ref[...] = x1


def _ffn_ln(x, wg, wu, wd, g, b, ple=None):
    t, d = x.shape
    hidden = wg.shape[1]
    assert t % FFN_TM == 0 and hidden % FFN_CHUNK == 0
    row = lambda w: pl.BlockSpec((FFN_TM, w), lambda i: (i, 0))
    in_specs = [row(d), _resident((d, hidden)), _resident((d, hidden)), _resident((hidden, d)),
                _resident((1, d)), _resident((1, d))]
    args = [x, wg, wu, wd, g, b]
    if ple is not None:
        p, wp, wpg, bpg, g2, b2 = ple
        pd = p.shape[1]
        in_specs += [row(pd), _resident((pd, d)), _resident((d, d)), _resident((1, d)),
                     _resident((1, d)), _resident((1, d))]
        args += [p, wp, wpg, bpg, g2, b2]
    return pl.pallas_call(
        functools.partial(_ffn_ln_kernel, with_ple=ple is not None),
        grid=(t // FFN_TM,),
        in_specs=in_specs,
        out_specs=row(d),
        out_shape=jax.ShapeDtypeStruct((t, d), F32),
        compiler_params=pltpu.CompilerParams(
            dimension_semantics=("arbitrary",), vmem_limit_bytes=VMEM_LIMIT),
        name="ffn_ln_ple" if ple is not None else "ffn_ln",
    )(*args)


_PROJ_GROUPS = ("qk_a", "v_a", "o_a", "q_b", "k_b", "v_b", "g_a", "g_b")
_PROJ_DTYPES = {"qk_a": F32, "v_a": BF16, "o_a": F32, "q_b": BF16, "k_b": BF16, "v_b": BF16,
                "g_a": F32, "g_b": F32}


def _in_proj_kernel(x_ref, w_ref, b_ref, wif_ref, bif_ref, wift_ref, bift_ref, *out_refs):
    xb = x_ref[...].astype(BF16)
    d = x_ref.shape[1]
    for gi, (name, o_ref) in enumerate(zip(_PROJ_GROUPS, out_refs)):
        c0 = gi * d
        y = _dot(xb, w_ref[:, c0:c0 + d]) + b_ref[:, c0:c0 + d]
        if name == "q_b":
            y = y * (LOG2_E * DH_B ** -0.5)
        o_ref[...] = y.astype(o_ref.dtype)
    ifc_ref, ifr_ref = out_refs[len(_PROJ_GROUPS):]
    ifc_ref[...] = _dot(xb, wif_ref[...]) + bif_ref[...]
    ifr_ref[...] = _dot_nt(wift_ref[...], xb) + bift_ref[...]


def _in_proj(x, w_main, b_main, w_if, b_if, w_ift, b_ift):
    t, d = x.shape
    assert t % PROJ_TM == 0
    n_main = w_main.shape[1]
    row = lambda w: pl.BlockSpec((PROJ_TM, w), lambda i: (i, 0))
    out_shape = [jax.ShapeDtypeStruct((t, d), _PROJ_DTYPES[k]) for k in _PROJ_GROUPS]
    out_specs = [row(d) for _ in _PROJ_GROUPS]
    out_shape += [jax.ShapeDtypeStruct((t, V7X_LANES), F32),
                  jax.ShapeDtypeStruct((V7X_SUBLANES, t), F32)]
    out_specs += [row(V7X_LANES), pl.BlockSpec((V7X_SUBLANES, PROJ_TM), lambda i: (0, i))]
    return pl.pallas_call(
        _in_proj_kernel,
        grid=(t // PROJ_TM,),
        in_specs=[row(d), _resident((d, n_main)), _resident((1, n_main)),
                  _resident((d, V7X_LANES)), _resident((1, V7X_LANES)),
                  _resident((V7X_SUBLANES, d)), _resident((V7X_SUBLANES, 1))],
        out_specs=out_specs,
        out_shape=out_shape,
        compiler_params=pltpu.CompilerParams(
            dimension_semantics=("arbitrary",), vmem_limit_bytes=VMEM_LIMIT),
        name="in_proj",
    )(x, w_main, b_main, w_if, b_if, w_ift, b_ift)


def _mlstm_kernel(qk_ref, v_ref, o_ref, ifc_ref, ifr_ref, cw_ref, cb_ref, bfc_ref, bfr_ref,
                  hg_ref, y_ref, ext_ref, q_s, k_s, c_s, n_s, m_s):
    ts = qk_ref.shape[1]
    lc = MLSTM_L
    halo = V7X_SUBLANES

    @pl.when(pl.program_id(1) == 0)
    def _():
        ext_ref[0:halo, :] = jnp.zeros((halo, ext_ref.shape[1]), F32)
        c_s[...] = jnp.zeros(c_s.shape, F32)
        n_s[...] = jnp.zeros(n_s.shape, F32)
        m_s[...] = jnp.zeros(m_s.shape, F32)

    ext_ref[halo:halo + ts, :] = qk_ref[0]
    conv = cb_ref[...]
    for j in range(CONV_W):
        off = halo - (CONV_W - 1) + j
        conv = conv + cw_ref[j:j + 1, :] * ext_ref[off:off + ts, :]
    ext_ref[0:halo, :] = ext_ref[ts:ts + halo, :]
    qk = conv * jax.nn.sigmoid(conv)
    half = qk.shape[1] // 2
    q_s[...] = qk[:, :half]
    k_s[...] = qk[:, half:] * (DK_A ** -0.5)

    ifc = ifc_ref[...]
    ifr = ifr_ref[...]
    logf_c = _log_sigmoid(ifc + bfc_ref[...])
    logf_r = _log_sigmoid(ifr + bfr_ref[...])

    t_ids = lax.broadcasted_iota(jnp.int32, (lc, lc), 0)
    s_ids = lax.broadcasted_iota(jnp.int32, (lc, lc), 1)
    causal = s_ids <= t_ids
    lower = causal.astype(BF16)
    upper = (t_ids <= s_ids).astype(BF16)

    for c in range(ts // lc):
        r0 = c * lc
        hi, mid, lo = _split3(logf_c[r0:r0 + lc, :])
        bcum_c = _dot(lower, hi) + _dot(lower, mid) + _dot(lower, lo)
        hi, mid, lo = _split3(logf_r[:, r0:r0 + lc])
        bcum_r = _dot(hi, upper) + _dot(mid, upper) + _dot(lo, upper)
        for h in range(H_A):
            q = q_s[r0:r0 + lc, h * DK_A:(h + 1) * DK_A]
            k = k_s[r0:r0 + lc, h * DK_A:(h + 1) * DK_A]
            v = v_ref[0, r0:r0 + lc, h * DV_A:(h + 1) * DV_A]
            qb = q.astype(BF16)
            fh = H_A + h
            bc_col = bcum_c[:, fh:fh + 1]
            bc_row = bcum_r[fh:fh + 1, :]
            i_col = ifc[r0:r0 + lc, h:h + 1]
            i_row = ifr[h:h + 1, r0:r0 + lc]
            c_st = c_s[h]
            n_st = n_s[h]
            m_st = m_s[h, 0:1, 0:1]

            dmat = jnp.where(causal, bc_col - bc_row + i_row, -jnp.inf)
            inter = bc_col + m_st
            m_t = jnp.maximum(inter, jnp.max(dmat, axis=1, keepdims=True))
            w_inter = jnp.exp(inter - m_t)
            scores = _dot_nt(qb, k.astype(BF16)) * jnp.exp(dmat - m_t)
            num = w_inter * _dot(qb, c_st.astype(BF16)) + _dot(scores.astype(BF16), v)
            den = (w_inter * jnp.sum(q * n_st, axis=1, keepdims=True)
                   + jnp.sum(scores, axis=1, keepdims=True))
            h_out = num * (1.0 / jnp.maximum(jnp.abs(den), jnp.exp(-m_t)))

            mu = jnp.mean(h_out, axis=1, keepdims=True)
            cen = h_out - mu
            var = jnp.mean(cen * cen, axis=1, keepdims=True)
            hn = cen * lax.rsqrt(var + LN_EPS) * hg_ref[h:h + 1, :]
            gate = jax.nn.sigmoid(o_ref[0, r0:r0 + lc, h * DV_A:(h + 1) * DV_A])
            y_ref[0, r0:r0 + lc, h * DV_A:(h + 1) * DV_A] = (hn * gate).astype(y_ref.dtype)

            b_last = bc_col[lc - 1:lc, :]
            g_row = b_last - bc_row + i_row
            g_col = b_last - bc_col + i_col
            m_new = jnp.maximum(b_last + m_st, jnp.max(g_row, axis=1, keepdims=True))
            a_prev = jnp.exp(b_last + m_st - m_new)
            kw = k * jnp.exp(g_col - m_new)
            c_s[h] = a_prev * c_st + _dot_tn(kw.astype(BF16), v)
            n_s[h] = a_prev * n_st + jnp.sum(kw, axis=0, keepdims=True)
            m_s[h] = jnp.broadcast_to(m_new, m_s.shape[1:])


def _mlstm(qk_pre, v_a, o_a, ifc, ifr, conv_w, conv_b, bf_col, bf_row, hn_g):
    b, s, wqk = qk_pre.shape
    wv = v_a.shape[2]
    ts = MLSTM_TS
    assert s % ts == 0 and ts % MLSTM_L == 0
    nblk = s // ts
    blk = lambda w: pl.BlockSpec((1, ts, w), lambda bi, si: (bi, si, 0))
    const = lambda shape: pl.BlockSpec(shape, lambda bi, si: (0,) * len(shape))
    return pl.pallas_call(
        _mlstm_kernel,
        grid=(b, nblk),
        in_specs=[blk(wqk), blk(wv), blk(wv),
                  pl.BlockSpec((ts, V7X_LANES), lambda bi, si: (bi * nblk + si, 0)),
                  pl.BlockSpec((V7X_SUBLANES, ts), lambda bi, si: (0, bi * nblk + si)),
                  const(conv_w.shape), const(conv_b.shape), const(bf_col.shape),
                  const(bf_row.shape), const(hn_g.shape)],
        out_specs=blk(wv),
        out_shape=jax.ShapeDtypeStruct((b, s, wv), BF16),
        scratch_shapes=[
            pltpu.VMEM((ts + 2 * V7X_SUBLANES, wqk), F32),
            pltpu.VMEM((ts, wqk // 2), F32),
            pltpu.VMEM((ts, wqk // 2), F32),
            pltpu.VMEM((H_A, DK_A, DV_A), F32),
            pltpu.VMEM((H_A, 1, DK_A), F32),
            pltpu.VMEM((H_A, V7X_SUBLANES, V7X_LANES), F32),
        ],
        compiler_params=pltpu.CompilerParams(
            dimension_semantics=("arbitrary", "arbitrary"), vmem_limit_bytes=VMEM_LIMIT),
        name="mlstm",
    )(qk_pre, v_a, o_a, ifc, ifr, conv_w, conv_b, bf_col, bf_row, hn_g)


def _sb_attn_kernel(q_ref, k_ref, v_ref, u2_ref, y_ref, out_s, acc_s, y_s, ls_s, hl_s, rs_s, w_s):
    tq, tk = ATT_TQ, ATT_TK
    qi = pl.program_id(2)
    q = q_ref[0]
    out_s[...] = jnp.zeros(out_s.shape, F32)
    acc_s[...] = jnp.zeros(acc_s.shape, F32)
    assert tq == 2 * tk
    top = 2 * qi + 1
    n_blocks = top + 1

    def key_start(n):
        return pl.multiple_of((top - n) * tk, tk)

    def scores(n, slot):
        y_s[slot] = _dot_nt(q, k_ref[0, pl.ds(key_start(n), tk), :])

    def front(n, slot, masked=False):
        y = y_s[slot]
        neg_abs = lax.bitcast_convert_type(
            lax.bitcast_convert_type(y, jnp.uint32) | jnp.uint32(0x80000000), F32)
        sp = jnp.maximum(y, 0.0) + jnp.log(1.0 + jnp.exp2(neg_abs)) * LOG2_E
        ls = y - sp
        if masked:
            t_pos = qi * tq + lax.broadcasted_iota(jnp.int32, (tq, tk), 0)
            s_pos = key_start(n) + lax.broadcasted_iota(jnp.int32, (tq, tk), 1)
            mask = s_pos < t_pos
            sp = jnp.where(mask, sp, 0.0)
            ls = jnp.where(mask, ls, MASKED_LOG2)
        hi = sp.astype(BF16)
        lo = (sp - hi.astype(F32)).astype(BF16)
        hl_s[slot, :, 0:tk] = hi
        hl_s[slot, :, tk:2 * tk] = lo
        ls_s[slot] = ls
        rs_s[slot] = jnp.sum(sp, axis=1, keepdims=True)

    def back(n, slot):
        del n
        acc = acc_s[...]
        later = _dot(hl_s[slot], u2_ref[...]) + acc
        w_s[slot] = jnp.exp2(ls_s[slot] - later).astype(BF16)
        acc_s[...] = acc + rs_s[slot]

    def pv(n, slot):
        out_s[...] += _dot(w_s[slot], v_ref[0, pl.ds(key_start(n), tk), :])

    def tick(t, parity, first=0, last=3):
        stages = ((pv, t - 3, 1 - parity), (back, t - 2, parity),
                  (front, t - 1, 1 - parity), (scores, t, parity))
        for stage, n, slot in stages[first:last + 1]:
            stage(n, slot)

    @pl.when(qi == 0)
    def _():
        scores(0, 0)
        scores(1, 1)
        front(0, 0, masked=True)
        front(1, 1, masked=True)
        back(0, 0)
        back(1, 1)
        pv(0, 0)
        pv(1, 1)

    @pl.when(qi > 0)
    def _():
        scores(0, 0)
        front(0, 0, masked=True)
        scores(1, 1)
        back(0, 0)
        front(1, 1, masked=True)
        scores(2, 0)
        tick(3, 1)

        def body(j, carry):
            t = 4 + 2 * j
            tick(t, 0)
            tick(t + 1, 1)
            return carry

        lax.fori_loop(0, qi - 1, body, 0)
        tick(n_blocks, 0, last=2)
        tick(n_blocks + 1, 1, last=1)
        tick(n_blocks + 2, 0, last=0)

    y_ref[0] = out_s[...].astype(y_ref.dtype)


def _sb_attn(q, k, v, u2):
    b, s, w = q.shape
    nh = w // DH_B
    assert s % ATT_TQ == 0 and ATT_TQ % ATT_TK == 0
    return pl.pallas_call(
        _sb_attn_kernel,
        grid=(b, nh, s // ATT_TQ),
        in_specs=[pl.BlockSpec((1, ATT_TQ, DH_B), lambda bi, hi, qi: (bi, qi, hi)),
                  pl.BlockSpec((1, s, DH_B), lambda bi, hi, qi: (bi, 0, hi)),
                  pl.BlockSpec((1, s, DH_B), lambda bi, hi, qi: (bi, 0, hi)),
                  pl.BlockSpec(u2.shape, lambda bi, hi, qi: (0, 0))],
        out_specs=pl.BlockSpec((1, ATT_TQ, DH_B), lambda bi, hi, qi: (bi, qi, hi)),
        out_shape=jax.ShapeDtypeStruct((b, s, w), BF16),
        scratch_shapes=[pltpu.VMEM((ATT_TQ, DH_B), F32),
                        pltpu.VMEM((ATT_TQ, 1), F32),
                        pltpu.VMEM((2, ATT_TQ, ATT_TK), F32),
                        pltpu.VMEM((2, ATT_TQ, ATT_TK), F32),
                        pltpu.VMEM((2, ATT_TQ, 2 * ATT_TK), BF16),
                        pltpu.VMEM((2, ATT_TQ, 1), F32),
                        pltpu.VMEM((2, ATT_TQ, ATT_TK), BF16)],
        compiler_params=pltpu.CompilerParams(
            dimension_semantics=("arbitrary", "arbitrary", "arbitrary"),
            vmem_limit_bytes=VMEM_LIMIT),
        name="sb_attn",
    )(q, k, v, u2)


def _merge_ln_kernel(x_ref, ya_ref, yb_ref, ga_ref, gb_ref, wa_ref, wb_ref, wo_ref, g_ref, b_ref,
                     o_ref):
    merged = (jax.nn.sigmoid(ga_ref[...]) * _dot(ya_ref[...], wa_ref[...])
              + jax.nn.sigmoid(gb_ref[...]) * _dot(yb_ref[...], wb_ref[...]))
    mix = _dot(merged.astype(BF16), wo_ref[...])
    o_ref[...] = _layer_norm(ALPHA * x_ref[...] + mix, g_ref[...], b_ref[...])


def _merge_ln(x, ya, yb, ga, gb, wa, wb, wo, g, b):
    t, d = x.shape
    assert t % MERGE_TM == 0
    row = pl.BlockSpec((MERGE_TM, d), lambda i: (i, 0))
    return pl.pallas_call(
        _merge_ln_kernel,
        grid=(t // MERGE_TM,),
        in_specs=[row] * 5 + [_resident((d, d))] * 3 + [_resident((1, d))] * 2,
        out_specs=row,
        out_shape=jax.ShapeDtypeStruct((t, d), F32),
        compiler_params=pltpu.CompilerParams(
            dimension_semantics=("arbitrary",), vmem_limit_bytes=VMEM_LIMIT),
        name="merge_ln",
    )(x, ya, yb, ga, gb, wa, wb, wo, g, b)


def _pad_cols(a, width):
    return jnp.pad(a, ((0, 0), (0, width - a.shape[1])))


def kernel(x, p, ln_g, ln_b, ffn_wg, ffn_wu, ffn_wd, w_in, b_in, b_forget, conv_w, conv_b, mh_norm_g, w_branch_a, w_branch_b, w_mix_out, w_ple, w_ple_gate, b_ple_gate):
    bsz, s, d = x.shape
    t = bsz * s
    a_qk, a_v = 2 * H_A * DK_A, H_A * DV_A
    n_if = 2 * H_A
    if0 = a_qk + 2 * a_v
    xt = x.reshape(t, d)

    tri = (jnp.arange(ATT_TK)[:, None] > jnp.arange(ATT_TK)[None, :]).astype(BF16)
    u2 = jnp.concatenate([tri, tri], axis=0)

    for i in range(DEPTH):
        vec = lambda a: a.reshape(1, -1)
        wbf = lambda a: a.astype(BF16)
        xt = _ffn_ln(xt, wbf(ffn_wg[i, 0]), wbf(ffn_wu[i, 0]), wbf(ffn_wd[i, 0]),
                     vec(ln_g[i, 0]), vec(ln_b[i, 0]))

        w_main = wbf(jnp.concatenate([w_in[i][:, :if0], w_in[i][:, if0 + n_if:]], axis=1))
        b_main = vec(jnp.concatenate([b_in[i][:if0], b_in[i][if0 + n_if:]]))
        w_if = w_in[i][:, if0:if0 + n_if]
        b_if = b_in[i][if0:if0 + n_if]
        proj = _in_proj(xt, w_main, b_main,
                        wbf(_pad_cols(w_if, V7X_LANES)), _pad_cols(vec(b_if), V7X_LANES),
                        wbf(w_if.T), b_if.reshape(n_if, 1))
        qk_a, v_a, o_a, q_b, k_b, v_b, g_a, g_b, ifc, ifr = proj

        zeros_h = jnp.zeros((H_A,), F32)
        bf_all = jnp.concatenate([zeros_h, b_forget[i]])
        seq = lambda a: a.reshape(bsz, s, a.shape[1])
        y_a = _mlstm(seq(qk_a), seq(v_a), seq(o_a), ifc, ifr, conv_w[i], vec(conv_b[i]),
                     _pad_cols(vec(bf_all), V7X_LANES), bf_all.reshape(n_if, 1), mh_norm_g[i])
        y_b = _sb_attn(seq(q_b), seq(k_b), seq(v_b), u2)

        xt = _merge_ln(xt, y_a.reshape(t, d), y_b.reshape(t, d), g_a, g_b,
                       wbf(w_branch_a[i]), wbf(w_branch_b[i]), wbf(w_mix_out[i]),
                       vec(ln_g[i, 1]), vec(ln_b[i, 1]))

        xt = _ffn_ln(xt, wbf(ffn_wg[i, 1]), wbf(ffn_wu[i, 1]), wbf(ffn_wd[i, 1]),
                     vec(ln_g[i, 2]), vec(ln_b[i, 2]),
                     ple=(p[i].reshape(t, -1), wbf(w_ple[i]), wbf(w_ple_gate[i]),
                          vec(b_ple_gate[i]), vec(ln_g[i, 3]), vec(ln_b[i, 3])))
    return xt.reshape(bsz, s, d)
```

```python
import functools

import jax
import jax.numpy as jnp
from jax import lax
from jax.experimental import pallas as pl
from jax.experimental.pallas import tpu as pltpu

F32 = jnp.float32
BF16 = jnp.bfloat16

DEPTH = 1
H_A, DK_A, DV_A = 4, 128, 256
CONV_W = 4
H_B, DH_B = 8, 128
LN_EPS = 1e-5
ALPHA = (2 * DEPTH) ** 0.25
LOG2_E = 1.4426950408889634
MASKED_LOG2 = -1e30

V7X_VMEM_BYTES = 64 * 1024 * 1024
V7X_LANES = 128
V7X_SUBLANES = 8
VMEM_LIMIT = V7X_VMEM_BYTES - 8 * 1024 * 1024

FFN_TM = 1024
FFN_CHUNK = 256
PROJ_TM = 512
MERGE_TM = 512
MLSTM_L = 128
MLSTM_TS = 256
ATT_TQ = 512
ATT_TK = 256


def _dot(a, b):
    return jnp.dot(a, b, preferred_element_type=F32)


def _dot_nt(a, b):
    return lax.dot_general(a, b, (((1,), (1,)), ((), ())), preferred_element_type=F32)


def _dot_tn(a, b):
    return lax.dot_general(a, b, (((0,), (0,)), ((), ())), preferred_element_type=F32)


def _layer_norm(r, g, b):
    mu = jnp.mean(r, axis=-1, keepdims=True)
    c = r - mu
    var = jnp.mean(c * c, axis=-1, keepdims=True)
    return c * lax.rsqrt(var + LN_EPS) * g + b


def _log_sigmoid(x):
    return jnp.minimum(x, 0.0) - jnp.log1p(jnp.exp(-jnp.abs(x)))


def _split3(x):
    hi = x.astype(BF16)
    r1 = x - hi.astype(F32)
    mid = r1.astype(BF16)
    lo = (r1 - mid.astype(F32)).astype(BF16)
    return hi, mid, lo


def _resident(shape):
    nd = len(shape)
    return pl.BlockSpec(shape, lambda *_: (0,) * nd, pipeline_mode=pl.Buffered(1))


def _ffn_ln_kernel(*refs, with_ple):
    if with_ple:
        (x_ref, wg_ref, wu_ref, wd_ref, g_ref, b_ref,
         p_ref, wp_ref, wpg_ref, bpg_ref, g2_ref, b2_ref, o_ref) = refs
    else:
        x_ref, wg_ref, wu_ref, wd_ref, g_ref, b_ref, o_ref = refs
    x = x_ref[...]
    xb = x.astype(BF16)
    hidden = wg_ref.shape[1]
    acc = None
    for c0 in range(0, hidden, FFN_CHUNK):
        gate = _dot(xb, wg_ref[:, c0:c0 + FFN_CHUNK])
        up = _dot(xb, wu_ref[:, c0:c0 + FFN_CHUNK])
        h = (gate * jax.nn.sigmoid(gate) * up).astype(BF16)
        y = _dot(h, wd_ref[c0:c0 + FFN_CHUNK, :])
        acc = y if acc is None else acc + y
    x1 = _layer_norm(ALPHA * x + 0.5 * acc, g_ref[...], b_ref[...])
    if with_ple:
        x1b = x1.astype(BF16)
        gate = jax.nn.sigmoid(_dot(x1b, wpg_ref[...]) + bpg_ref[...])
        emb = _dot(p_ref[...].astype(BF16), wp_ref[...])
        x1 = _layer_norm(ALPHA * x1 + gate * emb, g2_ref[...], b2_ref[...])
    o_ref[...] = x1


def _ffn_ln(x, wg, wu, wd, g, b, ple=None):
    t, d = x.shape
    hidden = wg.shape[1]
    assert t % FFN_TM == 0 and hidden % FFN_CHUNK == 0
    row = lambda w: pl.BlockSpec((FFN_TM, w), lambda i: (i, 0))
    in_specs = [row(d), _resident((d, hidden)), _resident((d, hidden)), _resident((hidden, d)),
                _resident((1, d)), _resident((1, d))]
    args = [x, wg, wu, wd, g, b]
    if ple is not None:
        p, wp, wpg, bpg, g2, b2 = ple
        pd = p.shape[1]
        in_specs += [row(pd), _resident((pd, d)), _resident((d, d)), _resident((1, d)),
                     _resident((1, d)), _resident((1, d))]
        args += [p, wp, wpg, bpg, g2, b2]
    return pl.pallas_call(
        functools.partial(_ffn_ln_kernel, with_ple=ple is not None),
        grid=(t // FFN_TM,),
        in_specs=in_specs,
        out_specs=row(d),
        out_shape=jax.ShapeDtypeStruct((t, d), F32),
        compiler_params=pltpu.CompilerParams(
            dimension_semantics=("arbitrary",), vmem_limit_bytes=VMEM_LIMIT),
        name="ffn_ln_ple" if ple is not None else "ffn_ln",
    )(*args)


_PROJ_GROUPS = ("qk_a", "v_a", "o_a", "q_b", "k_b", "v_b", "g_a", "g_b")
_PROJ_DTYPES = {"qk_a": F32, "v_a": BF16, "o_a": F32, "q_b": BF16, "k_b": BF16, "v_b": BF16,
                "g_a": F32, "g_b": F32}
_PROJ_HEAD_MAJOR = ("q_b", "k_b", "v_b")


def _in_proj_kernel(x_ref, w_ref, b_ref, wif_ref, bif_ref, wift_ref, bift_ref, *out_refs):
    xb = x_ref[...].astype(BF16)
    d = x_ref.shape[1]
    for gi, (name, o_ref) in enumerate(zip(_PROJ_GROUPS, out_refs)):
        c0 = gi * d
        y = _dot(xb, w_ref[:, c0:c0 + d]) + b_ref[:, c0:c0 + d]
        if name == "q_b":
            y = y * (LOG2_E * DH_B ** -0.5)
        if name in _PROJ_HEAD_MAJOR:
            for h in range(H_B):
                o_ref[0, h] = y[:, h * DH_B:(h + 1) * DH_B].astype(o_ref.dtype)
        else:
            o_ref[...] = y.astype(o_ref.dtype)
    ifc_ref, ifr_ref = out_refs[len(_PROJ_GROUPS):]
    ifc_ref[...] = _dot(xb, wif_ref[...]) + bif_ref[...]
    ifr_ref[...] = _dot_nt(wift_ref[...], xb) + bift_ref[...]


def _in_proj(x, w_main, b_main, w_if, b_if, w_ift, b_ift, seq):
    t, d = x.shape
    assert t % seq == 0 and seq % PROJ_TM == 0
    n_main = w_main.shape[1]
    spb = seq // PROJ_TM
    row = lambda w: pl.BlockSpec((PROJ_TM, w), lambda i: (i, 0))
    out_shape, out_specs = [], []
    for k in _PROJ_GROUPS:
        if k in _PROJ_HEAD_MAJOR:
            out_shape.append(jax.ShapeDtypeStruct((t // seq, H_B, seq, DH_B), _PROJ_DTYPES[k]))
            out_specs.append(pl.BlockSpec((1, H_B, PROJ_TM, DH_B),
                                          lambda i: (i // spb, 0, i % spb, 0)))
        else:
            out_shape.append(jax.ShapeDtypeStruct((t, d), _PROJ_DTYPES[k]))
            out_specs.append(row(d))
    out_shape += [jax.ShapeDtypeStruct((t, V7X_LANES), F32),
                  jax.ShapeDtypeStruct((V7X_SUBLANES, t), F32)]
    out_specs += [row(V7X_LANES), pl.BlockSpec((V7X_SUBLANES, PROJ_TM), lambda i: (0, i))]
    return pl.pallas_call(
        _in_proj_kernel,
        grid=(t // PROJ_TM,),
        in_specs=[row(d), _resident((d, n_main)), _resident((1, n_main)),
                  _resident((d, V7X_LANES)), _resident((1, V7X_LANES)),
                  _resident((V7X_SUBLANES, d)), _resident((V7X_SUBLANES, 1))],
        out_specs=out_specs,
        out_shape=out_shape,
        compiler_params=pltpu.CompilerParams(
            dimension_semantics=("arbitrary",), vmem_limit_bytes=VMEM_LIMIT),
        name="in_proj",
    )(x, w_main, b_main, w_if, b_if, w_ift, b_ift)


def _mlstm_kernel(qk_ref, v_ref, o_ref, ifc_ref, ifr_ref, cw_ref, cb_ref, bfc_ref, bfr_ref,
                  hg_ref, y_ref, ext_ref, q_s, k_s, c_s, n_s, m_s):
    ts = qk_ref.shape[1]
    lc = MLSTM_L
    halo = V7X_SUBLANES

    @pl.when(pl.program_id(1) == 0)
    def _():
        ext_ref[0:halo, :] = jnp.zeros((halo, ext_ref.shape[1]), F32)
        c_s[...] = jnp.zeros(c_s.shape, F32)
        n_s[...] = jnp.zeros(n_s.shape, F32)
        m_s[...] = jnp.zeros(m_s.shape, F32)

    ext_ref[halo:halo + ts, :] = qk_ref[0]
    conv = cb_ref[...]
    for j in range(CONV_W):
        off = halo - (CONV_W - 1) + j
        conv = conv + cw_ref[j:j + 1, :] * ext_ref[off:off + ts, :]
    ext_ref[0:halo, :] = ext_ref[ts:ts + halo, :]
    qk = conv * jax.nn.sigmoid(conv)
    half = qk.shape[1] // 2
    q_s[...] = qk[:, :half]
    k_s[...] = qk[:, half:] * (DK_A ** -0.5)

    ifc = ifc_ref[...]
    ifr = ifr_ref[...]
    logf_c = _log_sigmoid(ifc + bfc_ref[...])
    logf_r = _log_sigmoid(ifr + bfr_ref[...])

    t_ids = lax.broadcasted_iota(jnp.int32, (lc, lc), 0)
    s_ids = lax.broadcasted_iota(jnp.int32, (lc, lc), 1)
    causal = s_ids <= t_ids
    lower = causal.astype(BF16)
    upper = (t_ids <= s_ids).astype(BF16)

    for c in range(ts // lc):
        r0 = c * lc
        hi, mid, lo = _split3(logf_c[r0:r0 + lc, :])
        bcum_c = _dot(lower, hi) + _dot(lower, mid) + _dot(lower, lo)
        hi, mid, lo = _split3(logf_r[:, r0:r0 + lc])
        bcum_r = _dot(hi, upper) + _dot(mid, upper) + _dot(lo, upper)
        for h in range(H_A):
            q = q_s[r0:r0 + lc, h * DK_A:(h + 1) * DK_A]
            k = k_s[r0:r0 + lc, h * DK_A:(h + 1) * DK_A]
            v = v_ref[0, r0:r0 + lc, h * DV_A:(h + 1) * DV_A]
            qb = q.astype(BF16)
            fh = H_A + h
            bc_col = bcum_c[:, fh:fh + 1]
            bc_row = bcum_r[fh:fh + 1, :]
            i_col = ifc[r0:r0 + lc, h:h + 1]
            i_row = ifr[h:h + 1, r0:r0 + lc]
            c_st = c_s[h]
            n_st = n_s[h]
            m_st = m_s[h, 0:1, 0:1]

            dmat = jnp.where(causal, bc_col - bc_row + i_row, -jnp.inf)
            inter = bc_col + m_st
            m_t = jnp.maximum(inter, jnp.max(dmat, axis=1, keepdims=True))
            w_inter = jnp.exp(inter - m_t)
            scores = _dot_nt(qb, k.astype(BF16)) * jnp.exp(dmat - m_t)
            num = w_inter * _dot(qb, c_st.astype(BF16)) + _dot(scores.astype(BF16), v)
            den = (w_inter * jnp.sum(q * n_st, axis=1, keepdims=True)
                   + jnp.sum(scores, axis=1, keepdims=True))
            h_out = num * (1.0 / jnp.maximum(jnp.abs(den), jnp.exp(-m_t)))

            mu = jnp.mean(h_out, axis=1, keepdims=True)
            cen = h_out - mu
            var = jnp.mean(cen * cen, axis=1, keepdims=True)
            hn = cen * lax.rsqrt(var + LN_EPS) * hg_ref[h:h + 1, :]
            gate = jax.nn.sigmoid(o_ref[0, r0:r0 + lc, h * DV_A:(h + 1) * DV_A])
            y_ref[0, r0:r0 + lc, h * DV_A:(h + 1) * DV_A] = (hn * gate).astype(y_ref.dtype)

            b_last = bc_col[lc - 1:lc, :]
            g_row = b_last - bc_row + i_row
            g_col = b_last - bc_col + i_col
            m_new = jnp.maximum(b_last + m_st, jnp.max(g_row, axis=1, keepdims=True))
            a_prev = jnp.exp(b_last + m_st - m_new)
            kw = k * jnp.exp(g_col - m_new)
            c_s[h] = a_prev * c_st + _dot_tn(kw.astype(BF16), v)
            n_s[h] = a_prev * n_st + jnp.sum(kw, axis=0, keepdims=True)
            m_s[h] = jnp.broadcast_to(m_new, m_s.shape[1:])


def _mlstm(qk_pre, v_a, o_a, ifc, ifr, conv_w, conv_b, bf_col, bf_row, hn_g):
    b, s, wqk = qk_pre.shape
    wv = v_a.shape[2]
    ts = MLSTM_TS
    assert s % ts == 0 and ts % MLSTM_L == 0
    nblk = s // ts
    blk = lambda w: pl.BlockSpec((1, ts, w), lambda bi, si: (bi, si, 0))
    const = lambda shape: pl.BlockSpec(shape, lambda bi, si: (0,) * len(shape))
    return pl.pallas_call(
        _mlstm_kernel,
        grid=(b, nblk),
        in_specs=[blk(wqk), blk(wv), blk(wv),
                  pl.BlockSpec((ts, V7X_LANES), lambda bi, si: (bi * nblk + si, 0)),
                  pl.BlockSpec((V7X_SUBLANES, ts), lambda bi, si: (0, bi * nblk + si)),
                  const(conv_w.shape), const(conv_b.shape), const(bf_col.shape),
                  const(bf_row.shape), const(hn_g.shape)],
        out_specs=blk(wv),
        out_shape=jax.ShapeDtypeStruct((b, s, wv), BF16),
        scratch_shapes=[
            pltpu.VMEM((ts + 2 * V7X_SUBLANES, wqk), F32),
            pltpu.VMEM((ts, wqk // 2), F32),
            pltpu.VMEM((ts, wqk // 2), F32),
            pltpu.VMEM((H_A, DK_A, DV_A), F32),
            pltpu.VMEM((H_A, 1, DK_A), F32),
            pltpu.VMEM((H_A, V7X_SUBLANES, V7X_LANES), F32),
        ],
        compiler_params=pltpu.CompilerParams(
            dimension_semantics=("arbitrary", "arbitrary"), vmem_limit_bytes=VMEM_LIMIT),
        name="mlstm",
    )(qk_pre, v_a, o_a, ifc, ifr, conv_w, conv_b, bf_col, bf_row, hn_g)


def _sb_attn_kernel(q_ref, k_ref, v_ref, u_ref, y_ref, out_s, acc_s, y_s, ls_s, sp_s, rs_s, w_s):
    tq, tk = ATT_TQ, ATT_TK
    qi = pl.program_id(2)
    q = q_ref[0, 0]
    out_s[...] = jnp.zeros(out_s.shape, F32)
    acc_s[...] = jnp.zeros(acc_s.shape, F32)
    n_diag = tq // tk
    assert n_diag % 2 == 0
    n_blocks = (qi + 1) * n_diag

    def key_start(n):
        return pl.multiple_of((n_blocks - 1 - n) * tk, tk)

    def scores(n, slot):
        y_s[slot] = _dot_nt(q, k_ref[0, 0, pl.ds(key_start(n), tk), :])

    def front(n, slot, masked=False):
        y = y_s[slot]
        neg_abs = lax.bitcast_convert_type(
            lax.bitcast_convert_type(y, jnp.uint32) | jnp.uint32(0x80000000), F32)
        sp = jnp.maximum(y, 0.0) + jnp.log(1.0 + jnp.exp2(neg_abs)) * LOG2_E
        ls = y - sp
        if masked:
            t_pos = qi * tq + lax.broadcasted_iota(jnp.int32, (tq, tk), 0)
            s_pos = key_start(n) + lax.broadcasted_iota(jnp.int32, (tq, tk), 1)
            mask = s_pos < t_pos
            sp = jnp.where(mask, sp, 0.0)
            ls = jnp.where(mask, ls, MASKED_LOG2)
        sp_s[slot] = sp.astype(BF16)
        ls_s[slot] = ls
        rs_s[slot] = jnp.sum(sp, axis=1, keepdims=True)

    def back(n, slot):
        del n
        acc = acc_s[...]
        later = _dot(sp_s[slot], u_ref[...]) + acc
        w_s[slot] = jnp.exp2(ls_s[slot] - later).astype(BF16)
        acc_s[...] = acc + rs_s[slot]

    def pv(n, slot):
        out_s[...] += _dot(w_s[slot], v_ref[0, 0, pl.ds(key_start(n), tk), :])

    def tick(t, parity=None, last=3, limit=None):
        for i, stage in enumerate((pv, back, front, scores)[:last + 1]):
            n = t - 3 + i
            if parity is None:
                if n < 0 or (limit is not None and n >= limit):
                    continue
                slot = n % 2
            else:
                slot = (parity + 1 + i) % 2
            if stage is front and parity is None and n < n_diag:
                front(n, slot, masked=True)
            else:
                stage(n, slot)

    t0 = n_diag + 2

    @pl.when(qi == 0)
    def _():
        for t in range(n_diag + 3):
            tick(t, limit=n_diag)

    @pl.when(qi > 0)
    def _():
        for t in range(t0):
            tick(t)

        def body(j, carry):
            t = t0 + 2 * j
            tick(t, parity=0)
            tick(t + 1, parity=1)
            return carry

        lax.fori_loop(0, (n_blocks - t0) // 2, body, 0)
        tick(n_blocks, parity=0, last=2)
        tick(n_blocks + 1, parity=1, last=1)
        tick(n_blocks + 2, parity=0, last=0)

    y_ref[0] = out_s[...].astype(y_ref.dtype)


def _sb_attn(q, k, v, tri):
    b, nh, s, dh = q.shape
    assert s % ATT_TQ == 0 and ATT_TQ % ATT_TK == 0
    kv_spec = pl.BlockSpec((1, 1, s, dh), lambda bi, hi, qi: (bi, hi, 0, 0))
    return pl.pallas_call(
        _sb_attn_kernel,
        grid=(b, nh, s // ATT_TQ),
        in_specs=[pl.BlockSpec((1, 1, ATT_TQ, dh), lambda bi, hi, qi: (bi, hi, qi, 0)),
                  kv_spec, kv_spec,
                  pl.BlockSpec(tri.shape, lambda bi, hi, qi: (0, 0))],
        out_specs=pl.BlockSpec((1, ATT_TQ, dh), lambda bi, hi, qi: (bi, qi, hi)),
        out_shape=jax.ShapeDtypeStruct((b, s, nh * dh), BF16),
        scratch_shapes=[pltpu.VMEM((ATT_TQ, DH_B), F32),
                        pltpu.VMEM((ATT_TQ, 1), F32),
                        pltpu.VMEM((2, ATT_TQ, ATT_TK), F32),
                        pltpu.VMEM((2, ATT_TQ, ATT_TK), F32),
                        pltpu.VMEM((2, ATT_TQ, ATT_TK), BF16),
                        pltpu.VMEM((2, ATT_TQ, 1), F32),
                        pltpu.VMEM((2, ATT_TQ, ATT_TK), BF16)],
        compiler_params=pltpu.CompilerParams(
            dimension_semantics=("arbitrary", "arbitrary", "arbitrary"),
            vmem_limit_bytes=VMEM_LIMIT),
        name="sb_attn",
    )(q, k, v, tri)


def _merge_ln_kernel(x_ref, ya_ref, yb_ref, ga_ref, gb_ref, wa_ref, wb_ref, wo_ref, g_ref, b_ref,
                     o_ref):
    merged = (jax.nn.sigmoid(ga_ref[...]) * _dot(ya_ref[...], wa_ref[...])
              + jax.nn.sigmoid(gb_ref[...]) * _dot(yb_ref[...], wb_ref[...]))
    mix = _dot(merged.astype(BF16), wo_ref[...])
    o_ref[...] = _layer_norm(ALPHA * x_ref[...] + mix, g_ref[...], b_ref[...])


def _merge_ln(x, ya, yb, ga, gb, wa, wb, wo, g, b):
    t, d = x.shape
    assert t % MERGE_TM == 0
    row = pl.BlockSpec((MERGE_TM, d), lambda i: (i, 0))
    return pl.pallas_call(
        _merge_ln_kernel,
        grid=(t // MERGE_TM,),
        in_specs=[row] * 5 + [_resident((d, d))] * 3 + [_resident((1, d))] * 2,
        out_specs=row,
        out_shape=jax.ShapeDtypeStruct((t, d), F32),
        compiler_params=pltpu.CompilerParams(
            dimension_semantics=("arbitrary",), vmem_limit_bytes=VMEM_LIMIT),
        name="merge_ln",
    )(x, ya, yb, ga, gb, wa, wb, wo, g, b)


def _pad_cols(a, width):
    return jnp.pad(a, ((0, 0), (0, width - a.shape[1])))


def kernel(x, p, ln_g, ln_b, ffn_wg, ffn_wu, ffn_wd, w_in, b_in, b_forget, conv_w, conv_b, mh_norm_g, w_branch_a, w_branch_b, w_mix_out, w_ple, w_ple_gate, b_ple_gate):
    bsz, s, d = x.shape
    t = bsz * s
    a_qk, a_v = 2 * H_A * DK_A, H_A * DV_A
    n_if = 2 * H_A
    if0 = a_qk + 2 * a_v
    xt = x.reshape(t, d)

    tri = (jnp.arange(ATT_TK)[:, None] > jnp.arange(ATT_TK)[None, :]).astype(BF16)

    for i in range(DEPTH):
        vec = lambda a: a.reshape(1, -1)
        wbf = lambda a: a.astype(BF16)
        xt = _ffn_ln(xt, wbf(ffn_wg[i, 0]), wbf(ffn_wu[i, 0]), wbf(ffn_wd[i, 0]),
                     vec(ln_g[i, 0]), vec(ln_b[i, 0]))

        w_main = wbf(jnp.concatenate([w_in[i][:, :if0], w_in[i][:, if0 + n_if:]], axis=1))
        b_main = vec(jnp.concatenate([b_in[i][:if0], b_in[i][if0 + n_if:]]))
        w_if = w_in[i][:, if0:if0 + n_if]
        b_if = b_in[i][if0:if0 + n_if]
        proj = _in_proj(xt, w_main, b_main,
                        wbf(_pad_cols(w_if, V7X_LANES)), _pad_cols(vec(b_if), V7X_LANES),
                        wbf(w_if.T), b_if.reshape(n_if, 1), seq=s)
        qk_a, v_a, o_a, q_b, k_b, v_b, g_a, g_b, ifc, ifr = proj

        zeros_h = jnp.zeros((H_A,), F32)
        bf_all = jnp.concatenate([zeros_h, b_forget[i]])
        seq = lambda a: a.reshape(bsz, s, a.shape[1])
        y_a = _mlstm(seq(qk_a), seq(v_a), seq(o_a), ifc, ifr, conv_w[i], vec(conv_b[i]),
                     _pad_cols(vec(bf_all), V7X_LANES), bf_all.reshape(n_if, 1), mh_norm_g[i])
        y_b = _sb_attn(q_b, k_b, v_b, tri)

        xt = _merge_ln(xt, y_a.reshape(t, d), y_b.reshape(t, d), g_a, g_b,
                       wbf(w_branch_a[i]), wbf(w_branch_b[i]), wbf(w_mix_out[i]),
                       vec(ln_g[i, 1]), vec(ln_b[i, 1]))

        xt = _ffn_ln(xt, wbf(ffn_wg[i, 1]), wbf(ffn_wu[i, 1]), wbf(ffn_wd[i, 1]),
                     vec(ln_g[i, 2]), vec(ln_b[i, 2]),
                     ple=(p[i].reshape(t, -1), wbf(w_ple[i]), wbf(w_ple_gate[i]),
                          vec(b_ple_gate[i]), vec(ln_g[i, 3]), vec(ln_b[i, 3])))
    return xt.reshape(bsz, s, d)
```
